```python
import math
import jax, jax.numpy as jnp
from jax import lax
import numpy as np

D_MODEL = 1024
BATCH = 2
SEQ = 8192
DEPTH = 4

CTX_LEN = 256
GRID_W = 64
EPS = 1e-6
NEG_INF = -1e30

NA_HEADS = 4
NA_HEAD_DIM = D_MODEL // 16
NA_WIDTH = NA_HEADS * NA_HEAD_DIM
NA_WIN_ROWS = 8
NA_WIN_COLS = 16

RG_WIDTH = D_MODEL // 2
RG_BLOCKS = 8
RG_BLOCK_W = RG_WIDTH // RG_BLOCKS
RG_CONV_W = 4
RG_CONV_LEFT = 2
RG_C = 8.0

DIFF_HEADS = 4
DIFF_HEAD_DIM = D_MODEL // 32
DIFF_V_DIM = 2 * DIFF_HEAD_DIM
DIFF_WIDTH = DIFF_HEADS * DIFF_V_DIM
Q_BLOCK = 128
ROPE_BASE = 10000.0

MIX_WIDTH = NA_WIDTH + RG_WIDTH + DIFF_WIDTH
IN_SPLIT_SIZES = (NA_WIDTH, NA_WIDTH, NA_WIDTH, RG_WIDTH, RG_WIDTH, DIFF_WIDTH, DIFF_WIDTH, DIFF_WIDTH)
IN_COLS = sum(IN_SPLIT_SIZES)
IN_SPLIT_POINTS = tuple(sum(IN_SPLIT_SIZES[:i + 1]) for i in range(len(IN_SPLIT_SIZES) - 1))

N_GROUPS = 4
EXP_PER_GROUP = 8
N_EXPERTS = N_GROUPS * EXP_PER_GROUP
MOE_TOP_K = 2
D_EXPERT = D_MODEL // 2
MOE_BLOCK = 128

kernel_name = 'hybrid_flow_natten_rglru_diffattn_hmoe'


def rmsnorm(x, g):
    xf = x.astype(jnp.float32)
    y = xf * lax.rsqrt(jnp.mean(xf * xf, axis=-1, keepdims=True) + EPS)
    return (y * g.astype(jnp.float32)).astype(x.dtype)


def axial_rope_tables(n):
    t = jnp.arange(n)
    row = (t // GRID_W).astype(jnp.float32)
    col = (t % GRID_W).astype(jnp.float32)
    half = DIFF_HEAD_DIM // 2
    inv_freq = ROPE_BASE ** (-jnp.arange(0, half, 2, dtype=jnp.float32) / half)
    ang_r = row[:, None] * inv_freq
    ang_c = col[:, None] * inv_freq
    return jnp.cos(ang_r), jnp.sin(ang_r), jnp.cos(ang_c), jnp.sin(ang_c)


def _rotate(x, cos, sin):
    cos = cos[:, None, None, :].astype(x.dtype)
    sin = sin[:, None, None, :].astype(x.dtype)
    x1, x2 = jnp.split(x, 2, axis=-1)
    return jnp.concatenate([x1 * cos - x2 * sin, x1 * sin + x2 * cos], axis=-1)


def apply_axial_rope(x, tables):
    cos_r, sin_r, cos_c, sin_c = tables
    half = x.shape[-1] // 2
    return jnp.concatenate([_rotate(x[..., :half], cos_r, sin_r),
                            _rotate(x[..., half:], cos_c, sin_c)], axis=-1)


def neighbourhood_attention(q, k, v, k_ctx, v_ctx, rpb):
    B, S, _ = q.shape
    rows = S // GRID_W
    wr = min(NA_WIN_ROWS, rows)
    n_loc = wr * GRID_W
    grid = lambda t: t.reshape(B, rows, GRID_W, NA_HEADS, NA_HEAD_DIM)
    qg, kg, vg = grid(q), grid(k), grid(v)
    r = jnp.arange(rows)
    row_idx = jnp.clip(r - wr // 2, 0, rows - wr)[:, None] + jnp.arange(wr)
    kb = kg[:, row_idx].reshape(B, rows, n_loc, NA_HEADS, NA_HEAD_DIM)
    vb = vg[:, row_idx].reshape(B, rows, n_loc, NA_HEADS, NA_HEAD_DIM)
    col = jnp.arange(GRID_W)
    col_start = jnp.clip(col - NA_WIN_COLS // 2, 0, GRID_W - NA_WIN_COLS)
    kcol = col[None, :]
    in_win = (kcol >= col_start[:, None]) & (kcol < col_start[:, None] + NA_WIN_COLS)
    dc = jnp.clip(kcol - col[:, None], 1 - NA_WIN_COLS, NA_WIN_COLS - 1)
    dr = row_idx - r[:, None]
    bias = rpb[:, dr[:, None, :, None] + (NA_WIN_ROWS - 1), dc[None, :, None, :] + (NA_WIN_COLS - 1)]
    bias = bias.reshape(NA_HEADS, rows, GRID_W, n_loc).astype(jnp.float32)
    mask = jnp.broadcast_to(in_win[:, None, :], (GRID_W, wr, GRID_W)).reshape(GRID_W, n_loc)
    scale = NA_HEAD_DIM ** -0.5
    s_loc = jnp.einsum('brqhd,brkhd->bhrqk', qg, kb).astype(jnp.float32) * scale
    s_loc = jnp.where(mask, s_loc + bias, NEG_INF)
    s_ctx = jnp.einsum('brqhd,bkhd->bhrqk', qg, k_ctx).astype(jnp.float32) * scale
    p = jax.nn.softmax(jnp.concatenate([s_loc, s_ctx], axis=-1), axis=-1).astype(v.dtype)
    o = (jnp.einsum('bhrqk,brkhd->brqhd', p[..., :n_loc], vb)
         + jnp.einsum('bhrqk,bkhd->brqhd', p[..., n_loc:], v_ctx))
    return o.reshape(B, S, NA_WIDTH)


def ctx_attention(q, k, v):
    s = jnp.einsum('bqhd,bkhd->bhqk', q, k).astype(jnp.float32) * (q.shape[-1] ** -0.5)
    p = jax.nn.softmax(s, axis=-1).astype(v.dtype)
    return jnp.einsum('bhqk,bkhd->bqhd', p, v)


def short_conv(x, w, b):
    T = x.shape[1]
    xp = jnp.pad(x, ((0, 0), (RG_CONV_LEFT, RG_CONV_W - 1 - RG_CONV_LEFT), (0, 0)))
    return sum(xp[:, j:j + T] * w[j] for j in range(RG_CONV_W)) + b


def rglru_coeffs(xcv, w_r, b_r, w_i, b_i, lam):
    B, T, C = xcv.shape
    xh = xcv.reshape(B, T, RG_BLOCKS, RG_BLOCK_W)
    r = jax.nn.sigmoid((jnp.einsum('btnc,ncd->btnd', xh, w_r).reshape(B, T, C) + b_r).astype(jnp.float32))
    i = jax.nn.sigmoid((jnp.einsum('btnc,ncd->btnd', xh, w_i).reshape(B, T, C) + b_i).astype(jnp.float32))
    log_a = -RG_C * r * jax.nn.softplus(-lam.astype(jnp.float32))
    a = jnp.exp(log_a)
    bx = jnp.sqrt(-jnp.expm1(2.0 * log_a)) * (i * xcv.astype(jnp.float32))
    return a, bx


def _scan_combine(left, right):
    a_l, b_l = left
    a_r, b_r = right
    return a_l * a_r, a_r * b_l + b_r


def linear_scan(a, bx, h0):
    a_cum, h = lax.associative_scan(_scan_combine, (a, bx), axis=1)
    return h + a_cum * h0[:, None, :]


def rglru_bidir(xr, conv_w, conv_b, w_r, b_r, w_i, b_i, lam, h0):
    xcv = short_conv(xr, conv_w, conv_b)
    a_f, bx_f = rglru_coeffs(xcv, w_r[0], b_r[0], w_i[0], b_i[0], lam[0])
    h_f = linear_scan(a_f, bx_f, h0[0])
    a_b, bx_b = rglru_coeffs(xcv, w_r[1], b_r[1], w_i[1], b_i[1], lam[1])
    h_b = linear_scan(a_b[:, ::-1], bx_b[:, ::-1], h0[1])[:, ::-1]
    return h_f, h_b


def diff_attend(q, k, v, lam):
    s = jnp.einsum('bqhcd,bkhcd->bhcqk', q, k).astype(jnp.float32) * (DIFF_HEAD_DIM ** -0.5)
    p = jax.nn.softmax(s, axis=-1)
    p_diff = (p[:, :, 0] - lam * p[:, :, 1]).astype(v.dtype)
    return jnp.einsum('bhqk,bkhe->bqhe', p_diff, v)


def diff_attention_latent(q, k, v, k_ctx, v_ctx, lam):
    B, S = q.shape[0], q.shape[1]
    k_all = jnp.concatenate([k, k_ctx], axis=1)
    v_all = jnp.concatenate([v, v_ctx], axis=1)
    n_blk = S // Q_BLOCK
    qb = jnp.moveaxis(q.reshape(B, n_blk, Q_BLOCK, DIFF_HEADS, 2, DIFF_HEAD_DIM), 1, 0)
    o = lax.map(lambda qi: diff_attend(qi, k_all, v_all, lam), qb)
    return jnp.moveaxis(o, 0, 1).reshape(B, S, DIFF_HEADS, DIFF_V_DIM)


def grouped_expert_ffn(h, expert_id, weight, w1, w3, w2):
    N, D = h.shape
    A = N * MOE_TOP_K
    e_flat = expert_id.reshape(A)
    tok_flat = jnp.repeat(jnp.arange(N, dtype=jnp.int32), MOE_TOP_K)
    order = jnp.argsort(e_flat)
    e_sorted = e_flat[order]
    tok_sorted = tok_flat[order]
    w_sorted = weight.reshape(A)[order]
    counts = jnp.bincount(e_flat, length=N_EXPERTS)
    padded = (counts + MOE_BLOCK - 1) // MOE_BLOCK * MOE_BLOCK
    pad_end = jnp.cumsum(padded)
    pad_start = pad_end - padded
    start = jnp.cumsum(counts) - counts
    dest = pad_start[e_sorted] + jnp.arange(A) - start[e_sorted]
    n_blocks = -(-A // MOE_BLOCK) + N_EXPERTS
    n_rows = n_blocks * MOE_BLOCK
    row_tok = jnp.full((n_rows,), N, dtype=jnp.int32).at[dest].set(tok_sorted)
    blk_expert = jnp.minimum(jnp.searchsorted(pad_end, jnp.arange(n_blocks) * MOE_BLOCK, side='right'),
                             N_EXPERTS - 1)
    h_pad = jnp.concatenate([h, jnp.zeros((1, D), h.dtype)], axis=0)
    xb = h_pad[row_tok].reshape(n_blocks, MOE_BLOCK, D)

    def expert_block(args):
        xi, e = args
        return (jax.nn.silu(xi @ w1[e]) * (xi @ w3[e])) @ w2[e]

    yb = lax.map(expert_block, (xb, blk_expert)).reshape(n_rows, D)
    contrib = yb[dest] * w_sorted[:, None].astype(yb.dtype)
    return jnp.zeros((N, D), h.dtype).at[tok_sorted].add(contrib)


def hierarchical_moe(h, w_g, b_g, w_e, b_e, w1, w3, w2):
    N = h.shape[0]
    g_prob = jax.nn.softmax((h @ w_g + b_g).astype(jnp.float32), axis=-1)
    g_top, g_idx = lax.top_k(g_prob, 1)
    e_logits = (h @ w_e + b_e).astype(jnp.float32).reshape(N, N_GROUPS, EXP_PER_GROUP)
    e_logits = jnp.take_along_axis(e_logits, g_idx[:, :, None], axis=1)[:, 0]
    e_top, e_idx = lax.top_k(jax.nn.softmax(e_logits, axis=-1), MOE_TOP_K)
    weight = g_top * e_top / jnp.sum(e_top, axis=-1, keepdims=True)
    expert_id = g_idx * EXP_PER_GROUP + e_idx
    return grouped_expert_ffn(h, expert_id, weight, w1, w3, w2)


def setup_inputs(seed: int = 0) -> dict:
    key = jax.random.key(seed)
    ks = jax.random.split(key, 28)
    f32 = jnp.float32
    D = D_MODEL
    nrm = lambda k, shape, s: jax.random.normal(k, shape, f32) * s
    u = jax.random.uniform(ks[17], (DEPTH, 2, RG_WIDTH), f32, 0.9, 0.999)
    a = u ** (1.0 / RG_C)
    return {
        'x': nrm(ks[0], (BATCH, SEQ, D), 1.0),
        'c': nrm(ks[1], (BATCH, D), 1.0),
        'ctx': nrm(ks[2], (BATCH, CTX_LEN, D), 1.0),
        'c_ctx': nrm(ks[3], (D,), 1.0),
        'w_ada': nrm(ks[4], (DEPTH, D, 6 * D), 0.5 * D ** -0.5),
        'b_ada': nrm(ks[5], (DEPTH, 6 * D), 0.02),
        'g_mix': 1.0 + nrm(ks[6], (DEPTH, D), 0.02),
        'g_ffn': 1.0 + nrm(ks[7], (DEPTH, D), 0.02),
        'w_in': nrm(ks[8], (DEPTH, D, IN_COLS), D ** -0.5),
        'w_out': nrm(ks[9], (DEPTH, MIX_WIDTH, D), MIX_WIDTH ** -0.5),
        'na_rpb': nrm(ks[10], (DEPTH, NA_HEADS, 2 * NA_WIN_ROWS - 1, 2 * NA_WIN_COLS - 1), 0.1),
        'rg_conv_w': nrm(ks[11], (DEPTH, RG_CONV_W, RG_WIDTH), RG_CONV_W ** -0.5),
        'rg_conv_b': nrm(ks[12], (DEPTH, RG_WIDTH), 0.02),
        'rg_w_r': nrm(ks[13], (DEPTH, 2, RG_BLOCKS, RG_BLOCK_W, RG_BLOCK_W), RG_BLOCK_W ** -0.5),
        'rg_b_r': nrm(ks[14], (DEPTH, 2, RG_WIDTH), 0.02),
        'rg_w_i': nrm(ks[15], (DEPTH, 2, RG_BLOCKS, RG_BLOCK_W, RG_BLOCK_W), RG_BLOCK_W ** -0.5),
        'rg_b_i': nrm(ks[16], (DEPTH, 2, RG_WIDTH), 0.02),
        'rg_lambda': jnp.log(a) - jnp.log1p(-a),
        'diff_lambda': nrm(ks[18], (DEPTH, 4, DIFF_HEAD_DIM), 0.1),
        'diff_subln_g': 1.0 + nrm(ks[19], (DEPTH, DIFF_V_DIM), 0.02),
        'router_w_group': nrm(ks[20], (DEPTH, D, N_GROUPS), D ** -0.5),
        'router_b_group': nrm(ks[21], (DEPTH, N_GROUPS), 0.01),
        'router_w_expert': nrm(ks[22], (DEPTH, D, N_EXPERTS), D ** -0.5),
        'router_b_expert': nrm(ks[23], (DEPTH, N_EXPERTS), 0.01),
        'moe_w1': nrm(ks[24], (DEPTH, N_EXPERTS, D, D_EXPERT), D ** -0.5),
        'moe_w3': nrm(ks[25], (DEPTH, N_EXPERTS, D, D_EXPERT), D ** -0.5),
        'moe_w2': nrm(ks[26], (DEPTH, N_EXPERTS, D_EXPERT, D), D_EXPERT ** -0.5),
        'g_final': 1.0 + nrm(ks[27], (D,), 0.02),
    }


def reference(x, c, ctx, c_ctx, w_ada, b_ada, g_mix, g_ffn, w_in, w_out, na_rpb,
              rg_conv_w, rg_conv_b, rg_w_r, rg_b_r, rg_w_i, rg_b_i, rg_lambda,
              diff_lambda, diff_subln_g, router_w_group, router_b_group,
              router_w_expert, router_b_expert, moe_w1, moe_w3, moe_w2, g_final):
    B, S, D = x.shape
    L = ctx.shape[1]
    rope = axial_rope_tables(S)
    s_c = jax.nn.silu(c)
    s_cc = jax.nn.silu(c_ctx)
    xl, xc = x, ctx
    for l in range(DEPTH):
        last = l == DEPTH - 1
        sh1, sc1, ga1, sh2, sc2, ga2 = [m[:, None, :] for m in jnp.split(s_c @ w_ada[l] + b_ada[l], 6, axis=-1)]
        csh1, csc1, cga1, csh2, csc2, cga2 = jnp.split(s_cc @ w_ada[l] + b_ada[l], 6)
        hl = rmsnorm(xl, g_mix[l]) * (1 + sc1) + sh1
        hc = rmsnorm(xc, g_mix[l]) * (1 + csc1) + csh1
        qa, ka, va, xr, gr, qd, kd, vd = jnp.split(hl @ w_in[l], IN_SPLIT_POINTS, axis=-1)
        qa_c, ka_c, va_c, xr_c, gr_c, qd_c, kd_c, vd_c = jnp.split(hc @ w_in[l], IN_SPLIT_POINTS, axis=-1)

        ka_c = ka_c.reshape(B, L, NA_HEADS, NA_HEAD_DIM)
        va_c = va_c.reshape(B, L, NA_HEADS, NA_HEAD_DIM)
        y_na = neighbourhood_attention(qa, ka, va, ka_c, va_c, na_rpb[l])

        rg_args = (rg_conv_w[l], rg_conv_b[l], rg_w_r[l], rg_b_r[l], rg_w_i[l], rg_b_i[l], rg_lambda[l])
        hf_c, hb_c = rglru_bidir(xr_c, *rg_args, jnp.zeros((2, B, RG_WIDTH), jnp.float32))
        hf, hb = rglru_bidir(xr, *rg_args, jnp.stack([hf_c[:, -1], hb_c[:, 0]]))
        y_rg = (hf + hb).astype(xr.dtype) * jax.nn.gelu(gr)

        lam_init = 0.8 - 0.6 * math.exp(-0.3 * l)
        lvec = diff_lambda[l].astype(jnp.float32)
        lam = jnp.exp(jnp.sum(lvec[0] * lvec[1])) - jnp.exp(jnp.sum(lvec[2] * lvec[3])) + lam_init
        qd_h = apply_axial_rope(qd.reshape(B, S, DIFF_HEADS, 2, DIFF_HEAD_DIM), rope)
        kd_h = apply_axial_rope(kd.reshape(B, S, DIFF_HEADS, 2, DIFF_HEAD_DIM), rope)
        vd_h = vd.reshape(B, S, DIFF_HEADS, DIFF_V_DIM)
        kd_c = kd_c.reshape(B, L, DIFF_HEADS, 2, DIFF_HEAD_DIM)
        vd_c = vd_c.reshape(B, L, DIFF_HEADS, DIFF_V_DIM)
        y_diff = diff_attention_latent(qd_h, kd_h, vd_h, kd_c, vd_c, lam)
        y_diff = (rmsnorm(y_diff, diff_subln_g[l]) * (1 - lam_init)).reshape(B, S, DIFF_WIDTH)

        xl = xl + ga1 * (jnp.concatenate([y_na, y_rg, y_diff], axis=-1) @ w_out[l])
        hl2 = rmsnorm(xl, g_ffn[l]) * (1 + sc2) + sh2
        moe_args = (router_w_group[l], router_b_group[l], router_w_expert[l], router_b_expert[l],
                    moe_w1[l], moe_w3[l], moe_w2[l])
        if last:
            xl = xl + ga2 * hierarchical_moe(hl2.reshape(B * S, D), *moe_args).reshape(B, S, D)
        else:
            y_na_c = ctx_attention(qa_c.reshape(B, L, NA_HEADS, NA_HEAD_DIM), ka_c, va_c).reshape(B, L, NA_WIDTH)
            y_rg_c = (hf_c + hb_c).astype(xr_c.dtype) * jax.nn.gelu(gr_c)
            y_diff_c = diff_attend(qd_c.reshape(B, L, DIFF_HEADS, 2, DIFF_HEAD_DIM), kd_c, vd_c, lam)
            y_diff_c = (rmsnorm(y_diff_c, diff_subln_g[l]) * (1 - lam_init)).reshape(B, L, DIFF_WIDTH)
            xc = xc + cga1 * (jnp.concatenate([y_na_c, y_rg_c, y_diff_c], axis=-1) @ w_out[l])
            hc2 = rmsnorm(xc, g_ffn[l]) * (1 + csc2) + csh2
            y = hierarchical_moe(jnp.concatenate([hl2.reshape(B * S, D), hc2.reshape(B * L, D)], axis=0), *moe_args)
            xl = xl + ga2 * y[:B * S].reshape(B, S, D)
            xc = xc + cga2 * y[B * S:].reshape(B, L, D)
    return rmsnorm(xl, g_final)
```

```python
import functools
import math

import jax
import jax.numpy as jnp
from jax import lax
from jax.experimental import pallas as pl
from jax.experimental.pallas import tpu as pltpu

F32 = jnp.float32
BF16 = jnp.bfloat16
I32 = jnp.int32
HIGHEST = lax.Precision.HIGHEST

D = 1024
NB = 2
S = 8192
L = 256
T = S + L
DEPTH = 4
GRID_W = 64
ROWS = S // GRID_W
EPS = 1e-6
NEG = -1e30

NA_H = 4
NA_DH = 64
NA_W = 256
NA_WR = 8
NA_WC = 16
NA_BAND = NA_WR * GRID_W

RG_W = 512
RG_NBLK = 8
RG_BW = 64
RG_C = 8.0

DF_H = 4
DF_D = 32
DF_V = 64
DF_W = 256
ROPE_BASE = 10000.0

N_GROUPS = 4
EPG = 8
N_EXP = 32
D_EXP = 512
MOE_BLOCK = 128

TM = 256
NT = T // TM
NLAT = S // TM
N_TOK = NB * T
N_ASSIGN = N_TOK * 2
N_BLK = -(-N_ASSIGN // MOE_BLOCK) + N_EXP
N_ROWS = N_BLK * MOE_BLOCK
SUB = 8
LANE = 128
TK = 512
NKC = S // TK

VMEM_LIMIT = 56 * 1024 * 1024


def _cparams(sem):
    return pltpu.CompilerParams(dimension_semantics=sem, vmem_limit_bytes=VMEM_LIMIT)


def _rms(x):
    return x * lax.rsqrt(jnp.mean(x * x, axis=-1, keepdims=True) + EPS)


def _dot(a, b):
    return jnp.dot(a, b, preferred_element_type=F32)


def _dot_nt(a, b):
    return lax.dot_general(a, b, (((1,), (1,)), ((), ())), preferred_element_type=F32)


def _adaln_kernel(cv_ref, w_ref, b_ref, o_ref):
    cv = cv_ref[...]
    s = cv * jax.nn.sigmoid(cv)
    o_ref[0] = jnp.dot(s, w_ref[0], precision=HIGHEST, preferred_element_type=F32) + b_ref[0]


def _adaln(cv, w_ada, b_ada):
    tn = 1536
    return pl.pallas_call(
        _adaln_kernel,
        grid=(DEPTH, 6 * D // tn),
        in_specs=[pl.BlockSpec((SUB, D), lambda l, j: (0, 0)),
                  pl.BlockSpec((1, D, tn), lambda l, j: (l, 0, j)),
                  pl.BlockSpec((1, 1, tn), lambda l, j: (l, 0, j))],
        out_specs=pl.BlockSpec((1, SUB, tn), lambda l, j: (l, 0, j)),
        out_shape=jax.ShapeDtypeStruct((DEPTH, SUB, 6 * D), F32),
        compiler_params=_cparams(("arbitrary", "arbitrary")),
        name="adaln",
    )(cv, w_ada, b_ada.reshape(DEPTH, 1, 6 * D))


def _rope(v, rc, rs1, rs2):
    return v * rc + pltpu.roll(v, DF_W - 8, 1) * rs1 + pltpu.roll(v, 8, 1) * rs2


def _inproj_kernel(x_ref, mod_ref, g_ref, w_ref, rc_ref, rs1_ref, rs2_ref,
                   qa_ref, ka_ref, va_ref, xr_ref, gr_ref, qd_ref, kd_ref, vd_ref):
    mod = mod_ref[0, 0]
    h = (_rms(x_ref[0]) * g_ref[...] * (1.0 + mod[1:2]) + mod[0:1]).astype(BF16)

    def proj(lo, hi):
        return _dot(h, w_ref[:, lo:hi])

    qa_ref[0] = (proj(0, 256) * (NA_DH ** -0.5)).astype(BF16)
    ka_ref[0] = proj(256, 512).astype(BF16)
    va_ref[0] = proj(512, 768).astype(BF16)
    xr_ref[0] = proj(768, 1280)
    gr_ref[0] = proj(1280, 1792)
    rc, rs1, rs2 = rc_ref[...], rs1_ref[...], rs2_ref[...]
    qd_ref[0] = (_rope(proj(1792, 2048), rc, rs1, rs2) * (DF_D ** -0.5)).astype(BF16)
    kd_ref[0] = _rope(proj(2048, 2304), rc, rs1, rs2).astype(BF16)
    vd_ref[0] = proj(2304, 2560).astype(BF16)


def _seg(t):
    return t // NLAT


def _inproj(x, modt, g, w, rc, rs1, rs2):
    tile = lambda w_: pl.BlockSpec((1, TM, w_), lambda b, t: (b, t, 0))
    tab = pl.BlockSpec((TM, DF_W), lambda b, t: (t, 0))
    sd = lambda w_, dt: jax.ShapeDtypeStruct((NB, T, w_), dt)
    return pl.pallas_call(
        _inproj_kernel,
        grid=(NB, NT),
        in_specs=[tile(D),
                  pl.BlockSpec((1, 1, SUB, D), lambda b, t: (b, _seg(t), 0, 0)),
                  pl.BlockSpec((1, D), lambda b, t: (0, 0)),
                  pl.BlockSpec((D, 2560), lambda b, t: (0, 0)),
                  tab, tab, tab],
        out_specs=[tile(256), tile(256), tile(256), tile(512), tile(512), tile(256), tile(256), tile(256)],
        out_shape=[sd(256, BF16), sd(256, BF16), sd(256, BF16), sd(512, F32), sd(512, F32),
                   sd(256, BF16), sd(256, BF16), sd(256, BF16)],
        compiler_params=_cparams(("arbitrary", "arbitrary")),
        name="inproj",
    )(x, modt, g, w, rc, rs1, rs2)


def _na_kernel(q_ref, k_ref, v_ref, bias_ref, o_ref):
    t = pl.program_id(1)
    lane = lax.broadcasted_iota(I32, (1, NA_W), 1)
    kc = k_ref[0, S:T, :]
    vc = v_ref[0, S:T, :]
    head_masks = [(lane >= h * NA_DH) & (lane < (h + 1) * NA_DH) for h in range(NA_H)]

    @pl.when(t < NLAT)
    def _():
        for j in range(TM // GRID_W):
            r = t * (TM // GRID_W) + j
            st = jnp.clip(r - NA_WR // 2, 0, ROWS - NA_WR)
            oi = st - r + (NA_WR - 1)
            k0 = pl.multiple_of(st * GRID_W, GRID_W)
            kb = k_ref[0, pl.ds(k0, NA_BAND), :]
            vb = v_ref[0, pl.ds(k0, NA_BAND), :]
            qr = q_ref[0, j * GRID_W:(j + 1) * GRID_W, :]
            out = jnp.zeros((GRID_W, NA_W), F32)
            for h in range(NA_H):
                qm = jnp.where(head_masks[h], qr, jnp.zeros_like(qr))
                sl = _dot_nt(qm, kb) + bias_ref[oi, h]
                sc = _dot_nt(qm, kc)
                m = jnp.maximum(jnp.max(sl, axis=-1, keepdims=True), jnp.max(sc, axis=-1, keepdims=True))
                pl_ = jnp.exp(sl - m)
                pc = jnp.exp(sc - m)
                den = jnp.sum(pl_, axis=-1, keepdims=True) + jnp.sum(pc, axis=-1, keepdims=True)
                o = _dot(pl_.astype(BF16), vb) + _dot(pc.astype(BF16), vc)
                out = out + jnp.where(head_masks[h], o / den, 0.0)
            o_ref[0, j * GRID_W:(j + 1) * GRID_W, :] = out.astype(BF16)

    @pl.when(t == NLAT)
    def _():
        qr = q_ref[0]
        out = jnp.zeros((TM, NA_W), F32)
        for h in range(NA_H):
            qm = jnp.where(head_masks[h], qr, jnp.zeros_like(qr))
            sc = _dot_nt(qm, kc)
            m = jnp.max(sc, axis=-1, keepdims=True)
            pc = jnp.exp(sc - m)
            den = jnp.sum(pc, axis=-1, keepdims=True)
            out = out + jnp.where(head_masks[h], _dot(pc.astype(BF16), vc) / den, 0.0)
        o_ref[0] = out.astype(BF16)


def _na(qa, ka, va, bias):
    full = pl.BlockSpec((1, T, NA_W), lambda b, t: (b, 0, 0))
    tile = pl.BlockSpec((1, TM, NA_W), lambda b, t: (b, t, 0))
    return pl.pallas_call(
        _na_kernel,
        grid=(NB, NT),
        in_specs=[tile, full, full,
                  pl.BlockSpec((NA_WR, NA_H, GRID_W, NA_BAND), lambda b, t: (0, 0, 0, 0))],
        out_specs=tile,
        out_shape=jax.ShapeDtypeStruct((NB, T, NA_W), BF16),
        compiler_params=_cparams(("arbitrary", "arbitrary")),
        name="natten",
    )(qa, ka, va, bias)


def _na_bias_table(rpb):
    qc = jnp.arange(GRID_W)
    col_start = jnp.clip(qc - NA_WC // 2, 0, GRID_W - NA_WC)
    kcol = qc[None, :]
    in_win = (kcol >= col_start[:, None]) & (kcol < col_start[:, None] + NA_WC)
    dc = jnp.clip(kcol - qc[:, None], 1 - NA_WC, NA_WC - 1) + (NA_WC - 1)
    o = jnp.arange(NA_WR)
    dr = (o[:, None] - (NA_WR - 1)) + jnp.arange(NA_WR)[None, :] + (NA_WR - 1)
    tab = rpb[:, dr[:, None, :, None], dc[None, :, None, :]]
    tab = jnp.where(in_win[None, None, :, None, :], tab, NEG)
    return tab.transpose(1, 0, 2, 3, 4).reshape(NA_WR, NA_H, GRID_W, NA_BAND).astype(F32)


def _softplus(x):
    return jnp.maximum(x, 0.0) + jnp.log1p(jnp.exp(-jnp.abs(x)))


def _gelu(x):
    return 0.5 * x * (1.0 + jnp.tanh(0.7978845608028654 * (x + 0.044715 * x * x * x)))


def _rg_coeffs(chunk, xr_ref, xp_ref, xn_ref, cw_ref, cb_ref, wg_ref, bg_ref, lam_ref, ext_s, a_s, bx_s):
    is_ctx = chunk == NLAT
    prev_ok = jnp.logical_not(is_ctx | (chunk == 0))
    next_ok = jnp.logical_not(is_ctx | (chunk == NLAT - 1))
    ext_s[0:SUB, :] = jnp.where(prev_ok, xp_ref[0], 0.0)
    ext_s[SUB:SUB + TM, :] = xr_ref[0]
    ext_s[SUB + TM:2 * SUB + TM, :] = jnp.where(next_ok, xn_ref[0], 0.0)
    cw = cw_ref[...]
    xcv = cb_ref[...] + sum(ext_s[SUB - 2 + j:SUB - 2 + j + TM, :] * cw[j:j + 1] for j in range(4))
    gates = _dot(xcv.astype(BF16), wg_ref[...]) + bg_ref[...]
    r = jax.nn.sigmoid(gates[:, :RG_W])
    i = jax.nn.sigmoid(gates[:, RG_W:])
    log_a = (-RG_C) * r * _softplus(-lam_ref[...])
    a_s[...] = jnp.exp(log_a)
    th = jnp.tanh(log_a)
    bx_s[...] = jnp.sqrt(-2.0 * th / (1.0 - th)) * (i * xcv)


def _tile_scan(a, b, reverse):
    row = lax.broadcasted_iota(I32, a.shape, 0)
    for d in (1, 2, 4):
        if reverse:
            a_sh, b_sh, ok = pltpu.roll(a, SUB - d, 0), pltpu.roll(b, SUB - d, 0), row < SUB - d
        else:
            a_sh, b_sh, ok = pltpu.roll(a, d, 0), pltpu.roll(b, d, 0), row >= d
        b = jnp.where(ok, a * b_sh + b, b)
        a = jnp.where(ok, a * a_sh, a)
    return a, b


def _rg_scan(a_s, bx_s, h_ref, carry_s, reverse):
    last = 0 if reverse else SUB - 1

    def body(k, c):
        kk = (TM // SUB - 1 - k) if reverse else k
        r0 = pl.multiple_of(kk * SUB, SUB)
        a, b = _tile_scan(a_s[pl.ds(r0, SUB), :], bx_s[pl.ds(r0, SUB), :], reverse)
        h = b + a * c
        h_ref[pl.ds(r0, SUB), :] = h
        return jnp.broadcast_to(h[last:last + 1, :], (SUB, RG_W))

    carry_s[...] = lax.fori_loop(0, TM // SUB, body, carry_s[...])


def _rg_fwd_kernel(xr_ref, xp_ref, xn_ref, cw_ref, cb_ref, wg_ref, bg_ref, lam_ref, hf_ref,
                   ext_s, a_s, bx_s, h_s, carry_s):
    i = pl.program_id(1)
    chunk = (i + NLAT) % NT

    @pl.when(i == 0)
    def _():
        carry_s[...] = jnp.zeros_like(carry_s)

    _rg_coeffs(chunk, xr_ref, xp_ref, xn_ref, cw_ref, cb_ref, wg_ref, bg_ref, lam_ref, ext_s, a_s, bx_s)
    _rg_scan(a_s, bx_s, h_s, carry_s, reverse=False)
    hf_ref[0] = h_s[...]


def _rg_bwd_kernel(xr_ref, xp_ref, xn_ref, cw_ref, cb_ref, wg_ref, bg_ref, lam_ref, hf_ref, gr_ref, y_ref,
                   ext_s, a_s, bx_s, h_s, carry_s):
    i = pl.program_id(1)
    chunk = NLAT - i

    @pl.when(i == 0)
    def _():
        carry_s[...] = jnp.zeros_like(carry_s)

    _rg_coeffs(chunk, xr_ref, xp_ref, xn_ref, cw_ref, cb_ref, wg_ref, bg_ref, lam_ref, ext_s, a_s, bx_s)
    _rg_scan(a_s, bx_s, h_s, carry_s, reverse=True)
    y_ref[0] = ((hf_ref[0] + h_s[...]) * _gelu(gr_ref[0])).astype(BF16)


def _rg_specs(order):
    n8 = TM // SUB
    tile = pl.BlockSpec((1, TM, RG_W), lambda b, i: (b, order(i), 0))
    prev = pl.BlockSpec((1, SUB, RG_W), lambda b, i: (b, jnp.maximum(order(i) * n8 - 1, 0), 0))
    nxt = pl.BlockSpec((1, SUB, RG_W), lambda b, i: (b, jnp.minimum((order(i) + 1) * n8, T // SUB - 1), 0))
    const = lambda shp: pl.BlockSpec(shp, lambda b, i: tuple(0 for _ in shp))
    params = [const((4, RG_W)), const((1, RG_W)), const((RG_W, 2 * RG_W)), const((1, 2 * RG_W)), const((1, RG_W))]
    return tile, prev, nxt, params


_RG_SCRATCH = [pltpu.VMEM((TM + 2 * SUB, RG_W), F32), pltpu.VMEM((TM, RG_W), F32), pltpu.VMEM((TM, RG_W), F32),
               pltpu.VMEM((TM, RG_W), F32), pltpu.VMEM((SUB, RG_W), F32)]


def _rg_fwd(xr, cw, cb, wg, bg, lam):
    tile, prev, nxt, params = _rg_specs(lambda i: (i + NLAT) % NT)
    return pl.pallas_call(
        _rg_fwd_kernel,
        grid=(NB, NT),
        in_specs=[tile, prev, nxt] + params,
        out_specs=tile,
        out_shape=jax.ShapeDtypeStruct((NB, T, RG_W), F32),
        scratch_shapes=_RG_SCRATCH,
        compiler_params=_cparams(("arbitrary", "arbitrary")),
        name="rglru_fwd",
    )(xr, xr, xr, cw, cb, wg, bg, lam)


def _rg_bwd(xr, cw, cb, wg, bg, lam, hf, gr):
    tile, prev, nxt, params = _rg_specs(lambda i: NLAT - i)
    return pl.pallas_call(
        _rg_bwd_kernel,
        grid=(NB, NT),
        in_specs=[tile, prev, nxt] + params + [tile, tile],
        out_specs=tile,
        out_shape=jax.ShapeDtypeStruct((NB, T, RG_W), BF16),
        scratch_shapes=_RG_SCRATCH,
        compiler_params=_cparams(("arbitrary", "arbitrary")),
        name="rglru_bwd",
    )(xr, xr, xr, cw, cb, wg, bg, lam, hf, gr)


def _gate_weights(w_r, b_r, w_i, b_i):
    eye = jnp.eye(RG_NBLK, dtype=F32)
    dense = lambda w: (eye[:, None, :, None] * w[:, :, None, :]).reshape(RG_W, RG_W)
    wg = jnp.concatenate([dense(w_r), dense(w_i)], axis=1).astype(BF16)
    bg = jnp.concatenate([b_r, b_i]).reshape(1, 2 * RG_W)
    return wg, bg


def _diff_kernel(lam_init, lam_ref, g_ref, q_ref, kt_ref, ktc_ref, v_ref, o_ref, m_s, acc_s):
    t = pl.program_id(2)
    q = q_ref[0, 0].reshape(2 * TM, DF_V)
    m_s[...] = jnp.full_like(m_s, -jnp.inf)
    acc_s[...] = jnp.zeros_like(acc_s)

    def sweep(kt, v):
        s = _dot(q, kt)
        m_old = m_s[...]
        m_new = jnp.maximum(m_old, jnp.max(s, axis=-1, keepdims=True))
        p = jnp.exp(s - m_new).astype(BF16)
        acc_s[...] = jnp.exp(m_old - m_new) * acc_s[...] + _dot(p, v)
        m_s[...] = m_new

    @pl.when(t < NLAT)
    def _():
        def body(i, c):
            sweep(kt_ref[0, 0, i], v_ref[0, 0, pl.ds(pl.multiple_of(i * TK, TK), TK), :])
            return c
        lax.fori_loop(0, NKC, body, 0)

    sweep(ktc_ref[0, 0], v_ref[0, 0, S:T, :])

    acc = acc_s[...]
    o0 = acc[:TM, :DF_V] / acc[:TM, DF_V:DF_V + 1]
    o1 = acc[TM:, :DF_V] / acc[TM:, DF_V:DF_V + 1]
    lv = lam_ref[...]
    lam = (jnp.exp(jnp.sum(lv[0:1] * lv[1:2], axis=-1, keepdims=True))
           - jnp.exp(jnp.sum(lv[2:3] * lv[3:4], axis=-1, keepdims=True)) + lam_init)
    o = o0 - lam * o1
    o_ref[0, 0] = (_rms(o) * g_ref[...] * (1.0 - lam_init)).astype(BF16)


def _diff(lam_init, lam, g, qz, kt, ktc, v):
    return pl.pallas_call(
        functools.partial(_diff_kernel, lam_init),
        grid=(NB, DF_H, NT),
        in_specs=[pl.BlockSpec((4, DF_D), lambda b, h, t: (0, 0)),
                  pl.BlockSpec((1, DF_V), lambda b, h, t: (0, 0)),
                  pl.BlockSpec((1, 1, 2, TM, DF_V), lambda b, h, t: (b, h, 0, t, 0)),
                  pl.BlockSpec((1, 1, NKC, DF_V, TK), lambda b, h, t: (b, h, 0, 0, 0)),
                  pl.BlockSpec((1, 1, DF_V, L), lambda b, h, t: (b, h, 0, 0)),
                  pl.BlockSpec((1, 1, T, LANE), lambda b, h, t: (b, h, 0, 0))],
        out_specs=pl.BlockSpec((1, 1, TM, DF_V), lambda b, h, t: (b, h, t, 0)),
        out_shape=jax.ShapeDtypeStruct((NB, DF_H, T, DF_V), BF16),
        scratch_shapes=[pltpu.VMEM((2 * TM, 1), F32), pltpu.VMEM((2 * TM, LANE), F32)],
        compiler_params=_cparams(("arbitrary", "arbitrary", "arbitrary")),
        name="diffattn",
    )(lam, g, qz, kt, ktc, v)


def _diff_layouts(qd, kd, vd):
    heads = lambda a: a.reshape(NB, T, DF_H, DF_V).transpose(0, 2, 1, 3)
    q4 = heads(qd)
    lane = jnp.arange(DF_V)
    zero = jnp.zeros_like(q4)
    qz = jnp.stack([jnp.where(lane < DF_D, q4, zero), jnp.where(lane >= DF_D, q4, zero)], axis=2)
    k4 = heads(kd)
    kt = k4[:, :, :S].reshape(NB, DF_H, NKC, TK, DF_V).transpose(0, 1, 2, 4, 3)
    ktc = k4[:, :, S:].transpose(0, 1, 3, 2)
    v4 = heads(vd)
    pad = jnp.zeros((NB, DF_H, T, LANE - DF_V), BF16).at[..., 0].set(1.0)
    return qz, kt, ktc, jnp.concatenate([v4, pad], axis=-1)


def _outproj_kernel(yna_ref, yrg_ref, ydf_ref, x_ref, mod_ref, g_ref, wo_ref, wr_ref, br_ref,
                    xo_ref, h8_ref, rt_ref, cnt_ref, run_s):
    first = (pl.program_id(0) == 0) & (pl.program_id(1) == 0)

    @pl.when(first)
    def _():
        run_s[...] = jnp.zeros_like(run_s)

    mod = mod_ref[0, 0]
    mix = (_dot(yna_ref[0], wo_ref[0:256, :]) + _dot(yrg_ref[0], wo_ref[256:768, :])
           + _dot(ydf_ref[0], wo_ref[768:1024, :]))
    xn = x_ref[0] + mod[2:3] * mix
    xo_ref[0] = xn
    h2 = _rms(xn) * g_ref[...] * (1.0 + mod[4:5]) + mod[3:4]
    for s in range(SUB):
        h8_ref[pl.ds(s, TM, stride=SUB), :] = h2[:, s * LANE:(s + 1) * LANE]

    lg = jnp.dot(h2, wr_ref[...], precision=HIGHEST, preferred_element_type=F32) + br_ref[...]
    lane = lax.broadcasted_iota(I32, (TM, LANE), 1)
    big = jnp.int32(LANE)
    g_l = jnp.where(lane < N_GROUPS, lg, NEG)
    g_max = jnp.max(g_l, axis=-1, keepdims=True)
    g_top = 1.0 / jnp.sum(jnp.exp(g_l - g_max), axis=-1, keepdims=True)
    g_idx = jnp.min(jnp.where(g_l == g_max, lane, big), axis=-1, keepdims=True)
    e_mask = (lane >= N_GROUPS) & (lane < N_GROUPS + N_EXP) & (((lane - N_GROUPS) >> 3) == g_idx)
    e_l = jnp.where(e_mask, lg, NEG)
    e_exp = jnp.where(e_mask, jnp.exp(e_l - jnp.max(e_l, axis=-1, keepdims=True)), 0.0)
    e_p = e_exp / jnp.sum(e_exp, axis=-1, keepdims=True)
    e_p = jnp.where(e_mask, e_p, -1.0)
    p1 = jnp.max(e_p, axis=-1, keepdims=True)
    i1 = jnp.min(jnp.where(e_p == p1, lane, big), axis=-1, keepdims=True)
    e_p2 = jnp.where(lane == i1, -1.0, e_p)
    p2 = jnp.max(e_p2, axis=-1, keepdims=True)
    i2 = jnp.min(jnp.where(e_p2 == p2, lane, big), axis=-1, keepdims=True)
    wsum = p1 + p2
    e1 = i1 - N_GROUPS
    e2 = i2 - N_GROUPS

    oh1 = lane == e1
    oh2 = lane == e2
    oh = jnp.where(oh1 | oh2, 1.0, 0.0)
    rr = lax.broadcasted_iota(I32, (TM, TM), 0)
    cc = lax.broadcasted_iota(I32, (TM, TM), 1)
    before = _dot(jnp.where(cc < rr, 1.0, 0.0).astype(BF16), oh.astype(BF16)) + run_s[...]
    rank1 = jnp.sum(jnp.where(oh1, before, 0.0), axis=-1, keepdims=True)
    rank2 = jnp.sum(jnp.where(oh2, before, 0.0), axis=-1, keepdims=True)
    run_s[...] = run_s[...] + jnp.sum(oh, axis=0, keepdims=True)
    cnt_ref[...] = run_s[...]

    cols = (e1.astype(F32), e2.astype(F32), g_top * p1 / wsum, g_top * p2 / wsum, rank1, rank2)
    rt = jnp.zeros((TM, LANE), F32)
    for c, val in enumerate(cols):
        rt = jnp.where(lane == c, val, rt)
    rt_ref[0] = rt


def _outproj(yna, yrg, ydf, x, modt, g, wo, wr, br):
    tile = lambda w_: pl.BlockSpec((1, TM, w_), lambda b, t: (b, t, 0))
    const = lambda shp: pl.BlockSpec(shp, lambda b, t: tuple(0 for _ in shp))
    return pl.pallas_call(
        _outproj_kernel,
        grid=(NB, NT),
        in_specs=[tile(256), tile(512), tile(256), tile(D),
                  pl.BlockSpec((1, 1, SUB, D), lambda b, t: (b, _seg(t), 0, 0)),
                  const((1, D)), const((D, D)), const((D, LANE)), const((1, LANE))],
        out_specs=[tile(D),
                   pl.BlockSpec((TM * SUB, LANE), lambda b, t: (b * NT + t, 0)),
                   tile(LANE),
                   const((1, LANE))],
        out_shape=[jax.ShapeDtypeStruct((NB, T, D), F32),
                   jax.ShapeDtypeStruct((N_TOK * SUB, LANE), F32),
                   jax.ShapeDtypeStruct((NB, T, LANE), F32),
                   jax.ShapeDtypeStruct((1, LANE), F32)],
        scratch_shapes=[pltpu.VMEM((1, LANE), F32)],
        compiler_params=_cparams(("arbitrary", "arbitrary")),
        name="outproj_router",
    )(yna, yrg, ydf, x, modt, g, wo, wr, br)


def _moe_kernel(slot_ref, be_ref, nv_ref, h8_ref, w1_ref, w3_ref, w2_ref, y8_ref, xbuf, ybuf, sem_in, sem_out):
    i = pl.program_id(0)
    cur = i % 2

    def gather(blk, buf):
        def body(j, c):
            tok = slot_ref[blk * MOE_BLOCK + j] >> 1
            pltpu.make_async_copy(h8_ref.at[pl.ds(pl.multiple_of(tok * SUB, SUB), SUB), :],
                                  xbuf.at[buf, pl.ds(pl.multiple_of(j * SUB, SUB), SUB), :],
                                  sem_in.at[buf]).start()
            return c
        lax.fori_loop(0, nv_ref[blk], body, 0)

    def wait_rows(src, dst, sem, n):
        @pl.when(n > 0)
        def _():
            pltpu.make_async_copy(src.at[pl.ds(0, n * SUB), :], dst.at[pl.ds(0, n * SUB), :], sem).wait()

    @pl.when(i == 0)
    def _():
        xbuf[...] = jnp.zeros_like(xbuf)
        gather(0, 0)

    @pl.when(i + 1 < N_BLK)
    def _():
        gather(i + 1, 1 - cur)

    nv = nv_ref[i]
    wait_rows(h8_ref, xbuf.at[cur], sem_in.at[cur], nv)

    @pl.when(i >= 2)
    def _():
        wait_rows(ybuf.at[cur], y8_ref, sem_out.at[cur], nv_ref[jnp.maximum(i - 2, 0)])

    @pl.when(nv > 0)
    def _():
        h1 = jnp.zeros((MOE_BLOCK, D_EXP), F32)
        h3 = jnp.zeros((MOE_BLOCK, D_EXP), F32)
        for s in range(SUB):
            xs = xbuf[cur, pl.ds(s, MOE_BLOCK, stride=SUB), :].astype(BF16)
            h1 = h1 + _dot(xs, w1_ref[0, s * LANE:(s + 1) * LANE, :])
            h3 = h3 + _dot(xs, w3_ref[0, s * LANE:(s + 1) * LANE, :])
        act = (h1 * jax.nn.sigmoid(h1) * h3).astype(BF16)
        y = _dot(act, w2_ref[0])
        for s in range(SUB):
            ybuf[cur, pl.ds(s, MOE_BLOCK, stride=SUB), :] = y[:, s * LANE:(s + 1) * LANE]

        def body(j, c):
            sl = slot_ref[i * MOE_BLOCK + j]
            pltpu.make_async_copy(ybuf.at[cur, pl.ds(pl.multiple_of(j * SUB, SUB), SUB), :],
                                  y8_ref.at[pl.ds(pl.multiple_of(sl * SUB, SUB), SUB), :],
                                  sem_out.at[cur]).start()
            return c
        lax.fori_loop(0, nv, body, 0)

    @pl.when(i == N_BLK - 1)
    def _():
        wait_rows(ybuf.at[1 - cur], y8_ref, sem_out.at[1 - cur], nv_ref[jnp.maximum(i - 1, 0)])
        wait_rows(ybuf.at[cur], y8_ref, sem_out.at[cur], nv)


def _moe(slot, be, nv, h8, w1, w3, w2):
    grid_spec = pltpu.PrefetchScalarGridSpec(
        num_scalar_prefetch=3,
        grid=(N_BLK,),
        in_specs=[pl.BlockSpec(memory_space=pl.ANY),
                  pl.BlockSpec((1, D, D_EXP), lambda i, sl, be, nv: (be[i], 0, 0)),
                  pl.BlockSpec((1, D, D_EXP), lambda i, sl, be, nv: (be[i], 0, 0)),
                  pl.BlockSpec((1, D_EXP, D), lambda i, sl, be, nv: (be[i], 0, 0))],
        out_specs=pl.BlockSpec(memory_space=pl.ANY),
        scratch_shapes=[pltpu.VMEM((2, MOE_BLOCK * SUB, LANE), F32),
                        pltpu.VMEM((2, MOE_BLOCK * SUB, LANE), F32),
                        pltpu.SemaphoreType.DMA((2,)),
                        pltpu.SemaphoreType.DMA((2,))],
    )
    return pl.pallas_call(
        _moe_kernel,
        grid_spec=grid_spec,
        out_shape=jax.ShapeDtypeStruct((N_ASSIGN * SUB, LANE), F32),
        compiler_params=_cparams(("arbitrary",)),
        name="moe_experts",
    )(slot, be, nv, h8, w1, w3, w2)


def _dispatch_plan(route, counts):
    rt = route.reshape(N_TOK, LANE)
    e = rt[:, 0:2].astype(I32)
    rank = rt[:, 4:6].astype(I32)
    cnt = counts[0, :N_EXP].astype(I32)
    padded = (cnt + MOE_BLOCK - 1) // MOE_BLOCK * MOE_BLOCK
    pad_end = jnp.cumsum(padded)
    pad_start = pad_end - padded
    dest = pad_start[e] + rank
    slot = jnp.full((N_ROWS,), -1, I32).at[dest.reshape(-1)].set(jnp.arange(N_ASSIGN, dtype=I32))
    blk0 = jnp.arange(N_BLK, dtype=I32) * MOE_BLOCK
    be = jnp.minimum(jnp.searchsorted(pad_end, blk0, side='right'), N_EXP - 1).astype(I32)
    nv = jnp.clip(cnt[be] - (blk0 - pad_start[be]), 0, MOE_BLOCK).astype(I32)
    n_used = pad_end[-1] // MOE_BLOCK
    be = jnp.where(jnp.arange(N_BLK) < n_used, be, be[jnp.maximum(n_used - 1, 0)])
    return slot, be, nv


def _combine_kernel(final, y8_ref, rt_ref, x_ref, mod_ref, g_ref, o_ref):
    mod = mod_ref[0, 0]
    rt = rt_ref[0]
    w1 = rt[:, 2:3]
    w2 = rt[:, 3:4]
    for s in range(SUB):
        y1 = y8_ref[pl.ds(s, TM, stride=2 * SUB), :]
        y2 = y8_ref[pl.ds(SUB + s, TM, stride=2 * SUB), :]
        cols = slice(s * LANE, (s + 1) * LANE)
        o_ref[0, :, cols] = x_ref[0, :, cols] + mod[5:6, cols] * (w1 * y1 + w2 * y2)
    if final:
        o_ref[0] = _rms(o_ref[0]) * g_ref[...]


def _combine(final, y8, route, x, modt, g):
    nt = NLAT if final else NT
    tile = lambda w_: pl.BlockSpec((1, TM, w_), lambda b, t: (b, t, 0))
    return pl.pallas_call(
        functools.partial(_combine_kernel, final),
        grid=(NB, nt),
        in_specs=[pl.BlockSpec((TM * 2 * SUB, LANE), lambda b, t: (b * NT + t, 0)),
                  tile(LANE), tile(D),
                  pl.BlockSpec((1, 1, SUB, D), lambda b, t: (b, _seg(t), 0, 0)),
                  pl.BlockSpec((1, D), lambda b, t: (0, 0))],
        out_specs=tile(D),
        out_shape=jax.ShapeDtypeStruct((NB, nt * TM, D), F32),
        compiler_params=_cparams(("arbitrary", "arbitrary")),
        name="moe_combine_final" if final else "moe_combine",
    )(y8, route, x, modt, g)


def _rope_tables():
    t = jnp.arange(S)
    row = (t // GRID_W).astype(F32)
    col = (t % GRID_W).astype(F32)
    half = DF_D // 2
    inv_freq = ROPE_BASE ** (-jnp.arange(0, half, 2, dtype=F32) / half)
    lane = jnp.arange(DF_W)
    use_col = (lane % DF_D) >= half
    ang = jnp.where(use_col[None, :], col[:, None], row[:, None]) * inv_freq[lane % (half // 2)][None, :]
    first = (lane % half) < (half // 2)
    cos, sin = jnp.cos(ang), jnp.sin(ang)
    rc = cos
    rs1 = jnp.where(first[None, :], -sin, 0.0)
    rs2 = jnp.where(first[None, :], 0.0, sin)
    ident = lambda a, v: jnp.concatenate([a, jnp.full((L, DF_W), v, F32)], axis=0)
    return ident(rc, 1.0), ident(rs1, 0.0), ident(rs2, 0.0)


def _mod_table(mods):
    m6 = mods.reshape(DEPTH, SUB, 6, D)
    lat = m6[:, :NB]
    cx = jnp.broadcast_to(m6[:, NB:NB + 1], (DEPTH, NB, 6, D))
    modt = jnp.stack([lat, cx], axis=2)
    return jnp.pad(modt, ((0, 0), (0, 0), (0, 0), (0, SUB - 6), (0, 0)))


@jax.jit
def kernel(x, c, ctx, c_ctx, w_ada, b_ada, g_mix, g_ffn, w_in, w_out, na_rpb, rg_conv_w, rg_conv_b, rg_w_r, rg_b_r, rg_w_i, rg_b_i, rg_lambda, diff_lambda, diff_subln_g, router_w_group, router_b_group, router_w_expert, router_b_expert, moe_w1, moe_w3, moe_w2, g_final):
    cv = jnp.concatenate([c, c_ctx[None, :], jnp.zeros((SUB - NB - 1, D), F32)], axis=0)
    modt = _mod_table(_adaln(cv, w_ada, b_ada))
    rc, rs1, rs2 = _rope_tables()
    xa = jnp.concatenate([x, ctx], axis=1)

    for l in range(DEPTH):
        lam_init = 0.8 - 0.6 * math.exp(-0.3 * l)
        qa, ka, va, xr, gr, qd, kd, vd = _inproj(xa, modt[l], g_mix[l].reshape(1, D),
                                                 w_in[l][:, :2560].astype(BF16), rc, rs1, rs2)
        y_na = _na(qa, ka, va, _na_bias_table(na_rpb[l]))

        cw, cb = rg_conv_w[l], rg_conv_b[l].reshape(1, RG_W)
        wg_f, bg_f = _gate_weights(rg_w_r[l, 0], rg_b_r[l, 0], rg_w_i[l, 0], rg_b_i[l, 0])
        wg_b, bg_b = _gate_weights(rg_w_r[l, 1], rg_b_r[l, 1], rg_w_i[l, 1], rg_b_i[l, 1])
        hf = _rg_fwd(xr, cw, cb, wg_f, bg_f, rg_lambda[l, 0].reshape(1, RG_W))
        y_rg = _rg_bwd(xr, cw, cb, wg_b, bg_b, rg_lambda[l, 1].reshape(1, RG_W), hf, gr)

        qz, kt, ktc, vp = _diff_layouts(qd, kd, vd)
        y_df = _diff(lam_init, diff_lambda[l], diff_subln_g[l].reshape(1, DF_V), qz, kt, ktc, vp)
        y_df = y_df.transpose(0, 2, 1, 3).reshape(NB, T, DF_W)

        wr = jnp.zeros((D, LANE), F32).at[:, :N_GROUPS].set(router_w_group[l])
        wr = wr.at[:, N_GROUPS:N_GROUPS + N_EXP].set(router_w_expert[l])
        br = jnp.zeros((1, LANE), F32).at[0, :N_GROUPS].set(router_b_group[l])
        br = br.at[0, N_GROUPS:N_GROUPS + N_EXP].set(router_b_expert[l])
        xa, h8, route, counts = _outproj(y_na, y_rg, y_df, xa, modt[l], g_ffn[l].reshape(1, D),
                                         w_out[l].astype(BF16), wr, br)

        slot, be, nv = _dispatch_plan(route, counts)
        y8 = _moe(slot, be, nv, h8, moe_w1[l].astype(BF16), moe_w3[l].astype(BF16), moe_w2[l].astype(BF16))
        final = l == DEPTH - 1
        xa = _combine(final, y8, route, xa, modt[l], g_final.reshape(1, D))
    return xa
```
